```python
import jax, jax.numpy as jnp
from jax import lax
import numpy as np

D_MODEL = 1024
BATCH = 8
SEQ = 2048
DEPTH = 2
DEC_BATCH = 128
DEC_SEQ = 1
PAST_LEN = 16384
PAGE_SIZE = 128

MIX_WIDTH = D_MODEL
HEAD_DIM = 64
W_A = D_MODEL // 4
W_B = 3 * D_MODEL // 8
W_C = MIX_WIDTH - W_A - W_B
POOL_WINDOWS = (2, 4, 8, 16)
N_POOL_GROUPS = 4
POOL_GROUP = W_A // N_POOL_GROUPS
POOL_HIST = max(POOL_WINDOWS) - 1
CONV_B_WIDTH = 31
CONV_C_WIDTH = 3
PLE_DIM = 256
EPS = 1e-6
IN_COLS = 2 * W_A + 3 * W_B + 4 * W_C

kernel_name = "hybrid_pool_conformer_shortconv_step"


def _rmsnorm(x, g):
    xf = x.astype(jnp.float32)
    y = xf * lax.rsqrt(jnp.mean(xf * xf, axis=-1, keepdims=True) + EPS)
    return (y * g.astype(jnp.float32)).astype(x.dtype)


def _layernorm(x, g, b):
    xf = x.astype(jnp.float32)
    mu = jnp.mean(xf, axis=-1, keepdims=True)
    xc = xf - mu
    var = jnp.mean(xc * xc, axis=-1, keepdims=True)
    y = xc * lax.rsqrt(var + EPS) * g.astype(jnp.float32) + b.astype(jnp.float32)
    return y.astype(x.dtype)


def _causal_dwconv(hist, u, w):
    k, c = w.shape
    z = jnp.concatenate([hist.astype(u.dtype), u], axis=1)
    out = lax.conv_general_dilated(
        z, w.astype(u.dtype)[:, None, :], window_strides=(1,), padding='VALID',
        dimension_numbers=('NWC', 'WIO', 'NWC'), feature_group_count=c)
    return out, z[:, z.shape[1] - (k - 1):]


def _pool_mixer(hist, v, pos0, w_mix, scale):
    bsz, t_len, _ = v.shape
    z = jnp.concatenate([hist.astype(v.dtype), v], axis=1)
    zf = z.astype(jnp.float32)
    cs = jnp.concatenate([jnp.zeros((bsz, 1, W_A), jnp.float32),
                          jnp.cumsum(zf, axis=1)], axis=1)
    t = jnp.arange(t_len)
    outs = []
    for g, w in enumerate(POOL_WINDOWS):
        csg = cs[:, :, g * POOL_GROUP:(g + 1) * POOL_GROUP]
        s = csg[:, POOL_HIST + 1:POOL_HIST + 1 + t_len] - csg[:, POOL_HIST + 1 - w:POOL_HIST + 1 - w + t_len]
        cnt = jnp.minimum(w, pos0 + t + 1).astype(jnp.float32)
        outs.append(s / cnt[None, :, None])
    pooled = jnp.concatenate(outs, axis=-1) - v.astype(jnp.float32)
    pooled = pooled.reshape(bsz, t_len, N_POOL_GROUPS, POOL_GROUP)
    mixed = jnp.einsum('btgc,gcd->btgd', pooled, w_mix.astype(jnp.float32))
    mixed = mixed.reshape(bsz, t_len, W_A) * scale.astype(jnp.float32)
    return mixed.astype(v.dtype), z[:, z.shape[1] - POOL_HIST:]


def _layer(x, pe, h_pool, h_conv, h_sconv, pos0, norm_g, w_in, w_pool_mix, pool_scale,
           conv_b_w, conv_b_b, ln_b_g, ln_b_b, sconv_w, w_out, w_ple, w_ple_gate):
    h = _rmsnorm(x, norm_g)
    proj = h @ w_in
    sizes = [W_A, W_A, W_B, W_B, W_B, W_C, W_C, W_C, W_C]
    idx = [int(i) for i in np.cumsum(sizes)[:-1]]
    v_a, z_a, a_b, g_b, z_b, x_c, b_c, c_c, z_c = jnp.split(proj, idx, axis=-1)
    y_a, n_pool = _pool_mixer(h_pool, v_a, pos0, w_pool_mix, pool_scale)
    y_a = y_a * jax.nn.silu(z_a)
    u_b = a_b * jax.nn.sigmoid(g_b)
    c_b, n_conv = _causal_dwconv(h_conv, u_b, conv_b_w)
    c_b = c_b + conv_b_b
    y_b = jax.nn.silu(_layernorm(c_b, ln_b_g, ln_b_b)) * jax.nn.silu(z_b)
    u_c = c_c * x_c
    s_c, n_sconv = _causal_dwconv(h_sconv, u_c, sconv_w)
    y_c = b_c * s_c * jax.nn.silu(z_c)
    y = jnp.concatenate([y_a, y_b, y_c], axis=-1) @ w_out
    x = x + y
    gate = jax.nn.sigmoid((x @ w_ple_gate).astype(jnp.float32)).astype(x.dtype)
    x = x + (pe @ w_ple) * gate
    return x, n_pool, n_conv, n_sconv


def setup_inputs(seed: int = 0) -> dict:
    key = jax.random.key(seed)
    ks = jax.random.split(key, 24)
    f32 = jnp.float32
    nrm = lambda k, s, sc: jax.random.normal(k, s, f32) * sc
    return {
        "x_prompt": nrm(ks[0], (BATCH, SEQ, D_MODEL), 1.0),
        "x_sample": nrm(ks[1], (DEC_BATCH, DEC_SEQ, D_MODEL), 1.0),
        "state_pool": nrm(ks[2], (DEPTH, DEC_BATCH, POOL_HIST, W_A), 1.0),
        "state_conv": nrm(ks[3], (DEPTH, DEC_BATCH, CONV_B_WIDTH - 1, W_B), 0.5),
        "state_sconv": nrm(ks[4], (DEPTH, DEC_BATCH, CONV_C_WIDTH - 1, W_C), 1.0),
        "p_prompt": nrm(ks[5], (DEPTH, BATCH, SEQ, PLE_DIM), 1.0),
        "p_sample": nrm(ks[6], (DEPTH, DEC_BATCH, DEC_SEQ, PLE_DIM), 1.0),
        "norm_g": 1.0 + nrm(ks[7], (DEPTH, D_MODEL), 0.02),
        "w_in": nrm(ks[8], (DEPTH, D_MODEL, IN_COLS), D_MODEL ** -0.5),
        "w_pool_mix": nrm(ks[9], (DEPTH, N_POOL_GROUPS, POOL_GROUP, POOL_GROUP), POOL_GROUP ** -0.5),
        "pool_scale": 1.0 + nrm(ks[10], (DEPTH, W_A), 0.02),
        "conv_b_w": nrm(ks[11], (DEPTH, CONV_B_WIDTH, W_B), CONV_B_WIDTH ** -0.5),
        "conv_b_b": nrm(ks[12], (DEPTH, W_B), 0.02),
        "ln_b_g": 1.0 + nrm(ks[13], (DEPTH, W_B), 0.02),
        "ln_b_b": nrm(ks[14], (DEPTH, W_B), 0.02),
        "sconv_w": nrm(ks[15], (DEPTH, CONV_C_WIDTH, W_C), CONV_C_WIDTH ** -0.5),
        "w_out": nrm(ks[16], (DEPTH, MIX_WIDTH, D_MODEL), MIX_WIDTH ** -0.5),
        "w_ple": nrm(ks[17], (DEPTH, PLE_DIM, D_MODEL), PLE_DIM ** -0.5),
        "w_ple_gate": nrm(ks[18], (DEPTH, D_MODEL, D_MODEL), D_MODEL ** -0.5),
        "final_norm_g": 1.0 + nrm(ks[19], (D_MODEL,), 0.02),
    }


def reference(x_prompt, x_sample, state_pool, state_conv, state_sconv, p_prompt, p_sample,
              norm_g, w_in, w_pool_mix, pool_scale, conv_b_w, conv_b_b, ln_b_g, ln_b_b,
              sconv_w, w_out, w_ple, w_ple_gate, final_norm_g):
    xp, xs = x_prompt, x_sample
    bp = x_prompt.shape[0]
    dt = x_prompt.dtype
    pool_p, pool_s, conv_p, conv_s, sconv_p, sconv_s = [], [], [], [], [], []
    for i in range(DEPTH):
        lw = (norm_g[i], w_in[i], w_pool_mix[i], pool_scale[i], conv_b_w[i], conv_b_b[i],
              ln_b_g[i], ln_b_b[i], sconv_w[i], w_out[i], w_ple[i], w_ple_gate[i])
        xp, a, b, c = _layer(
            xp, p_prompt[i],
            jnp.zeros((bp, POOL_HIST, W_A), dt),
            jnp.zeros((bp, CONV_B_WIDTH - 1, W_B), dt),
            jnp.zeros((bp, CONV_C_WIDTH - 1, W_C), dt),
            0, *lw)
        pool_p.append(a); conv_p.append(b); sconv_p.append(c)
        xs, a, b, c = _layer(xs, p_sample[i], state_pool[i], state_conv[i], state_sconv[i],
                             PAST_LEN, *lw)
        pool_s.append(a); conv_s.append(b); sconv_s.append(c)
    y_prompt = _rmsnorm(xp, final_norm_g)
    y_sample = _rmsnorm(xs, final_norm_g)
    return (y_prompt, y_sample,
            jnp.stack(pool_p), jnp.stack(pool_s),
            jnp.stack(conv_p), jnp.stack(conv_s),
            jnp.stack(sconv_p), jnp.stack(sconv_s))
```

```python
import functools

import jax
import jax.numpy as jnp
from jax import lax
from jax.experimental import pallas as pl
from jax.experimental.pallas import tpu as pltpu

D_MODEL = 1024
DEPTH = 2
W_A = 256
W_B = 384
W_C = 384
PLE_DIM = 256
POOL_HIST = 15
CONV_B_WIDTH = 31
CONV_C_WIDTH = 3
POOL_GROUP = 64
EPS = 1e-6
PAST_LEN = 16384

O_VA, O_ZA, O_AB, O_GB, O_ZB, O_XC, O_BC, O_CC, O_ZC = (
    0, 256, 512, 896, 1280, 1664, 2048, 2432, 2816)
IN_COLS = 3200
Y_A, Y_B, Y_C = 0, W_A, W_A + W_B

HP = 16
HC = 32
HS = 8

ROWS = 32
V7X_VMEM_LIMIT = 58 * 1024 * 1024

F32 = jnp.float32
BF16 = jnp.bfloat16


def _sigmoid(x):
    return 1.0 / (1.0 + jnp.exp(-x))


def _silu(x):
    return x * _sigmoid(x)


def _pool_windows(lane_block):
    return (2, 4) if lane_block == 0 else (8, 16)


def _pool_select(lo, hi):
    lane = lax.broadcasted_iota(jnp.int32, lo.shape, 1)
    return jnp.where(lane < POOL_GROUP, lo, hi)


def _branch_b_tail(c, bias, ln_g, ln_b, z_b):
    c = c + bias
    mu = jnp.mean(c, axis=-1, keepdims=True)
    xc = c - mu
    var = jnp.mean(xc * xc, axis=-1, keepdims=True)
    ln = xc * lax.rsqrt(var + EPS) * ln_g + ln_b
    return _silu(ln) * _silu(z_b)


def _rmsnorm(x, g):
    ms = jnp.mean(x * x, axis=-1, keepdims=True)
    return x * lax.rsqrt(ms + EPS) * g


def _prompt_kernel(x_ref, pe_ref, ng_ref, win_ref, wmix_ref, pscale_ref, cbw_ref, cbb_ref,
                   lng_ref, lnb_ref, scw_ref, wout_ref, wple_ref, wgate_ref, fg_ref,
                   y_ref, pool_ref, conv_ref, sconv_ref,
                   xbuf, hbuf, proj, zp, zc, zs, pbuf, ycat):
    j = pl.program_id(1)
    nj = pl.num_programs(1)
    tile = x_ref.shape[0]
    n_chunks = tile // ROWS

    @pl.when(j == 0)
    def _zero_history():
        zp[:, 0:HP, :] = jnp.zeros((DEPTH, HP, W_A), F32)
        zc[:, 0:HC, :] = jnp.zeros((DEPTH, HC, W_B), F32)
        zs[:, 0:HS, :] = jnp.zeros((DEPTH, HS, W_C), F32)

    pos0 = j * tile

    for l in range(DEPTH):
        src = x_ref if l == 0 else xbuf
        g = ng_ref[l]
        for c in range(n_chunks):
            r = slice(c * ROWS, (c + 1) * ROWS)
            hbuf[r, :] = _rmsnorm(src[r, :], g).astype(BF16)

        zp[l, HP:HP + tile, :] = jnp.dot(hbuf[...], win_ref[l, :, O_VA:O_ZA],
                                         preferred_element_type=F32)
        proj[...] = jnp.dot(hbuf[...], win_ref[l, :, O_ZA:IN_COLS],
                            preferred_element_type=F32)
        po = O_ZA

        for c in range(n_chunks):
            r0 = c * ROWS
            t = pos0 + r0 + lax.broadcasted_iota(jnp.int32, (ROWS, 128), 0)
            for lb in range(2):
                cols = slice(lb * 128, (lb + 1) * 128)
                w_lo, w_hi = _pool_windows(lb)
                v = zp[l, HP + r0:HP + r0 + ROWS, cols]
                acc = v
                sums = {}
                for i in range(1, w_hi):
                    acc = acc + zp[l, HP + r0 - i:HP + r0 - i + ROWS, cols]
                    if i + 1 in (w_lo, w_hi):
                        sums[i + 1] = acc
                cnt_lo = jnp.minimum(w_lo, t + 1).astype(F32)
                cnt_hi = jnp.minimum(w_hi, t + 1).astype(F32)
                pooled = _pool_select(sums[w_lo] / cnt_lo, sums[w_hi] / cnt_hi) - v
                pbuf[r0:r0 + ROWS, cols] = pooled.astype(BF16)
        mixed = jnp.dot(pbuf[...], wmix_ref[l], preferred_element_type=F32)
        z_a = proj[:, O_ZA - po:O_AB - po]
        ycat[:, Y_A:Y_B] = (mixed * pscale_ref[l] * _silu(z_a)).astype(BF16)

        for c in range(n_chunks):
            r = slice(c * ROWS, (c + 1) * ROWS)
            a_b = proj[r, O_AB - po:O_GB - po]
            g_b = proj[r, O_GB - po:O_ZB - po]
            zc[l, HC + c * ROWS:HC + (c + 1) * ROWS, :] = a_b * _sigmoid(g_b)
        for c in range(n_chunks):
            r0 = c * ROWS
            base = HC + r0 - (CONV_B_WIDTH - 1)
            acc = cbw_ref[l, 0:1, :] * zc[l, base:base + ROWS, :]
            for k in range(1, CONV_B_WIDTH):
                acc = acc + cbw_ref[l, k:k + 1, :] * zc[l, base + k:base + k + ROWS, :]
            z_b = proj[r0:r0 + ROWS, O_ZB - po:O_XC - po]
            y_b = _branch_b_tail(acc, cbb_ref[l], lng_ref[l], lnb_ref[l], z_b)
            ycat[r0:r0 + ROWS, Y_B:Y_C] = y_b.astype(BF16)

        for c in range(n_chunks):
            r = slice(c * ROWS, (c + 1) * ROWS)
            x_c = proj[r, O_XC - po:O_BC - po]
            c_c = proj[r, O_CC - po:O_ZC - po]
            zs[l, HS + c * ROWS:HS + (c + 1) * ROWS, :] = c_c * x_c
        for c in range(n_chunks):
            r0 = c * ROWS
            base = HS + r0 - (CONV_C_WIDTH - 1)
            s = scw_ref[l, 0:1, :] * zs[l, base:base + ROWS, :]
            for k in range(1, CONV_C_WIDTH):
                s = s + scw_ref[l, k:k + 1, :] * zs[l, base + k:base + k + ROWS, :]
            b_c = proj[r0:r0 + ROWS, O_BC - po:O_CC - po]
            z_c = proj[r0:r0 + ROWS, O_ZC - po:IN_COLS - po]
            ycat[r0:r0 + ROWS, Y_C:D_MODEL] = (b_c * s * _silu(z_c)).astype(BF16)

        xbuf[...] = src[...] + jnp.dot(ycat[...], wout_ref[l], preferred_element_type=F32)

        hbuf[...] = xbuf[...].astype(BF16)
        gate = jnp.dot(hbuf[...], wgate_ref[l], preferred_element_type=F32)
        emb = jnp.dot(pe_ref[l].astype(BF16), wple_ref[l], preferred_element_type=F32)
        xbuf[...] = xbuf[...] + emb * _sigmoid(gate)

        @pl.when(j == nj - 1)
        def _write_state():
            pool_ref[l] = zp[l, tile + HP - POOL_HIST:tile + HP, :]
            conv_ref[l] = zc[l, tile + HC - (CONV_B_WIDTH - 1):tile + HC, :]
            sconv_ref[l] = zs[l, tile + HS - (CONV_C_WIDTH - 1):tile + HS, :]

        zp[l, 0:HP, :] = zp[l, tile:tile + HP, :]
        zc[l, 0:HC, :] = zc[l, tile:tile + HC, :]
        zs[l, 0:HS, :] = zs[l, tile:tile + HS, :]

    fg = fg_ref[...]
    for c in range(n_chunks):
        r = slice(c * ROWS, (c + 1) * ROWS)
        y_ref[r, :] = _rmsnorm(xbuf[r, :], fg)


def _full_spec(arr):
    nd = arr.ndim
    return pl.BlockSpec(arr.shape, lambda b, j: (0,) * nd, pipeline_mode=pl.Buffered(1))


def _prompt_call(x, pe, weights, tile):
    batch, seq, _ = x.shape
    nj = seq // tile
    in_specs = [
        pl.BlockSpec((None, tile, D_MODEL), lambda b, j: (b, j, 0)),
        pl.BlockSpec((DEPTH, None, tile, PLE_DIM), lambda b, j: (0, b, j, 0)),
    ] + [_full_spec(w) for w in weights]
    out_shape = (
        jax.ShapeDtypeStruct((batch, seq, D_MODEL), F32),
        jax.ShapeDtypeStruct((DEPTH, batch, POOL_HIST, W_A), F32),
        jax.ShapeDtypeStruct((DEPTH, batch, CONV_B_WIDTH - 1, W_B), F32),
        jax.ShapeDtypeStruct((DEPTH, batch, CONV_C_WIDTH - 1, W_C), F32),
    )
    out_specs = (
        pl.BlockSpec((None, tile, D_MODEL), lambda b, j: (b, j, 0)),
        pl.BlockSpec((DEPTH, None, POOL_HIST, W_A), lambda b, j: (0, b, 0, 0)),
        pl.BlockSpec((DEPTH, None, CONV_B_WIDTH - 1, W_B), lambda b, j: (0, b, 0, 0)),
        pl.BlockSpec((DEPTH, None, CONV_C_WIDTH - 1, W_C), lambda b, j: (0, b, 0, 0)),
    )
    scratch = [
        pltpu.VMEM((tile, D_MODEL), F32),
        pltpu.VMEM((tile, D_MODEL), BF16),
        pltpu.VMEM((tile, IN_COLS - O_ZA), F32),
        pltpu.VMEM((DEPTH, HP + tile, W_A), F32),
        pltpu.VMEM((DEPTH, HC + tile, W_B), F32),
        pltpu.VMEM((DEPTH, HS + tile, W_C), F32),
        pltpu.VMEM((tile, W_A), BF16),
        pltpu.VMEM((tile, D_MODEL), BF16),
    ]
    return pl.pallas_call(
        _prompt_kernel,
        grid=(batch, nj),
        in_specs=in_specs,
        out_specs=out_specs,
        out_shape=out_shape,
        scratch_shapes=scratch,
        compiler_params=pltpu.CompilerParams(
            dimension_semantics=("arbitrary", "arbitrary"),
            vmem_limit_bytes=V7X_VMEM_LIMIT),
        name="prompt_trunk",
    )(x, pe, *weights)


def _sample_kernel(x_ref, pe_ref, sp_ref, sc_ref, ss_ref, ng_ref, win_ref, wmix_ref, pscale_ref,
                   cbw_ref, cbb_ref, lng_ref, lnb_ref, scw_ref, wout_ref, wple_ref, wgate_ref,
                   fg_ref, y_ref, va_ref, ub_ref, uc_ref):
    x = x_ref[...]
    rows = x.shape[0]
    for l in range(DEPTH):
        h = _rmsnorm(x, ng_ref[l]).astype(BF16)
        proj = jnp.dot(h, win_ref[l], preferred_element_type=F32)
        v_a = proj[:, O_VA:O_ZA]
        va_ref[l] = v_a
        pooled = []
        for lb in range(2):
            cols = slice(lb * 128, (lb + 1) * 128)
            w_lo, w_hi = _pool_windows(lb)
            v = v_a[:, cols]
            acc = v
            sums = {}
            for i in range(1, w_hi):
                acc = acc + sp_ref[l, POOL_HIST - i, :, cols]
                if i + 1 in (w_lo, w_hi):
                    sums[i + 1] = acc
            cnt_lo = float(min(w_lo, PAST_LEN + 1))
            cnt_hi = float(min(w_hi, PAST_LEN + 1))
            pooled.append(_pool_select(sums[w_lo] / cnt_lo, sums[w_hi] / cnt_hi) - v)
        pooled = jnp.concatenate(pooled, axis=-1).astype(BF16)
        mixed = jnp.dot(pooled, wmix_ref[l], preferred_element_type=F32)
        y_a = mixed * pscale_ref[l] * _silu(proj[:, O_ZA:O_AB])
        u_b = proj[:, O_AB:O_GB] * _sigmoid(proj[:, O_GB:O_ZB])
        ub_ref[l] = u_b
        acc = cbw_ref[l, CONV_B_WIDTH - 1:CONV_B_WIDTH, :] * u_b
        for k in range(CONV_B_WIDTH - 1):
            acc = acc + cbw_ref[l, k:k + 1, :] * sc_ref[l, k]
        y_b = _branch_b_tail(acc, cbb_ref[l], lng_ref[l], lnb_ref[l], proj[:, O_ZB:O_XC])
        u_c = proj[:, O_CC:O_ZC] * proj[:, O_XC:O_BC]
        uc_ref[l] = u_c
        s = scw_ref[l, CONV_C_WIDTH - 1:CONV_C_WIDTH, :] * u_c
        for k in range(CONV_C_WIDTH - 1):
            s = s + scw_ref[l, k:k + 1, :] * ss_ref[l, k]
        y_c = proj[:, O_BC:O_CC] * s * _silu(proj[:, O_ZC:IN_COLS])
        ycat = jnp.concatenate([y_a, y_b, y_c], axis=-1).astype(BF16)
        x = x + jnp.dot(ycat, wout_ref[l], preferred_element_type=F32)
        gate = jnp.dot(x.astype(BF16), wgate_ref[l], preferred_element_type=F32)
        emb = jnp.dot(pe_ref[l].astype(BF16), wple_ref[l], preferred_element_type=F32)
        x = x + emb * _sigmoid(gate)
    y_ref[...] = _rmsnorm(x, fg_ref[...])
    del rows


def _sample_call(x, pe, sp, sc, ss, weights):
    rows = x.shape[0]
    out_shape = (
        jax.ShapeDtypeStruct((rows, D_MODEL), F32),
        jax.ShapeDtypeStruct((DEPTH, rows, W_A), F32),
        jax.ShapeDtypeStruct((DEPTH, rows, W_B), F32),
        jax.ShapeDtypeStruct((DEPTH, rows, W_C), F32),
    )
    n_in = 5 + len(weights)
    vmem = pl.BlockSpec(memory_space=pltpu.VMEM)
    return pl.pallas_call(
        _sample_kernel,
        in_specs=[vmem] * n_in,
        out_specs=(vmem,) * 4,
        out_shape=out_shape,
        compiler_params=pltpu.CompilerParams(vmem_limit_bytes=V7X_VMEM_LIMIT),
        name="decode_trunk",
    )(x, pe, sp, sc, ss, *weights)


def _block_diag(w_mix):
    depth, groups, gc, _ = w_mix.shape
    eye = jnp.eye(groups, dtype=w_mix.dtype)
    bd = jnp.einsum('lgcd,gh->lgchd', w_mix, eye)
    return bd.reshape(depth, groups * gc, groups * gc)


def kernel(x_prompt, x_sample, state_pool, state_conv, state_sconv, p_prompt, p_sample, norm_g, w_in, w_pool_mix, pool_scale, conv_b_w, conv_b_b, ln_b_g, ln_b_b, sconv_w, w_out, w_ple, w_ple_gate, final_norm_g):
    row = lambda a: a.reshape(DEPTH, 1, a.shape[-1])
    weights = (
        row(norm_g), w_in.astype(BF16), _block_diag(w_pool_mix).astype(BF16), row(pool_scale),
        conv_b_w, row(conv_b_b), row(ln_b_g), row(ln_b_b), sconv_w,
        w_out.astype(BF16), w_ple.astype(BF16), w_ple_gate.astype(BF16),
        final_norm_g.reshape(1, D_MODEL),
    )
    y_prompt, pool_p, conv_p, sconv_p = _prompt_call(x_prompt, p_prompt, weights, tile=256)

    dec = x_sample.shape[0]
    tm = lambda s: jnp.transpose(s, (0, 2, 1, 3))
    y_s, va, ub, uc = _sample_call(
        x_sample.reshape(dec, D_MODEL), p_sample.reshape(DEPTH, dec, PLE_DIM),
        tm(state_pool), tm(state_conv), tm(state_sconv), weights)
    pool_s = jnp.concatenate([state_pool[:, :, 1:], va[:, :, None, :]], axis=2)
    conv_s = jnp.concatenate([state_conv[:, :, 1:], ub[:, :, None, :]], axis=2)
    sconv_s = jnp.concatenate([state_sconv[:, :, 1:], uc[:, :, None, :]], axis=2)
    return (y_prompt, y_s.reshape(dec, 1, D_MODEL), pool_p, pool_s, conv_p, conv_s,
            sconv_p, sconv_s)
```
